```python
import jax, jax.numpy as jnp
from jax import lax
import numpy as np

D_MODEL = 1024
BATCH = 4
SEQ = 8192
DEPTH = 1
DEC_BATCH = 32
DEC_SEQ = 16
PAST_LEN = 4096

CHUNK = 64
LEFT_CHUNKS = 8
BAND_CHUNKS = LEFT_CHUNKS + 1
HEAD_DIM = 64
N_HEADS_A = 8
N_HEADS_B = 8
D_A = N_HEADS_A * HEAD_DIM
D_B = N_HEADS_B * HEAD_DIM
D_MIX = D_A + D_B
N_IDX_HEADS = 8
IDX_DIM = 32
TOPK_MAX = 256
MAX_REL = 128
N_REL = 2 * MAX_REL + 1
D_FF = 2816
ROPE_THETA = 10000.0
EPS = 1e-6
Q_BLOCK = 128
PROJ_SIZES = (D_A, D_A, D_A, D_B, D_B, D_B, N_IDX_HEADS * IDX_DIM, IDX_DIM, N_IDX_HEADS)
D_IN = sum(PROJ_SIZES)

kernel_name = "hybrid_chunk_band_dsa_streaming_step"


def _rmsnorm(x, g):
    xf = x.astype(jnp.float32)
    y = xf * lax.rsqrt(jnp.mean(xf * xf, axis=-1, keepdims=True) + EPS)
    return (y * g.astype(jnp.float32)).astype(x.dtype)


def _swiglu(h, wg, wu, wd):
    return (jax.nn.silu(h @ wg) * (h @ wu)) @ wd


def _macaron_ffn(x, g, wg, wu, wd):
    return x + 0.5 * _swiglu(_rmsnorm(x, g), wg, wu, wd)


def _rope(x, pos):
    half = x.shape[-1] // 2
    inv = ROPE_THETA ** (-jnp.arange(half, dtype=jnp.float32) / half)
    ang = pos.astype(jnp.float32)[:, None] * inv[None, :]
    cos, sin = jnp.cos(ang), jnp.sin(ang)
    if x.ndim == 4:
        cos, sin = cos[:, None, :], sin[:, None, :]
    xf = x.astype(jnp.float32)
    x1, x2 = xf[..., :half], xf[..., half:]
    return jnp.concatenate([x1 * cos - x2 * sin, x2 * cos + x1 * sin], axis=-1).astype(x.dtype)


def _mixer_inputs(x, pos, g, w_in):
    b, n, _ = x.shape
    p = _rmsnorm(x, g) @ w_in
    cuts = [int(c) for c in np.cumsum(PROJ_SIZES)[:-1]]
    qa, ka, va, qb, kb, vb, qi, ki, wi = jnp.split(p, cuts, axis=-1)
    hA = lambda t: t.reshape(b, n, N_HEADS_A, HEAD_DIM)
    hB = lambda t: t.reshape(b, n, N_HEADS_B, HEAD_DIM)
    qi = qi.reshape(b, n, N_IDX_HEADS, IDX_DIM)
    return (hA(qa), hA(ka), hA(va), _rope(hB(qb), pos), _rope(hB(kb), pos), hB(vb),
            _rope(qi, pos), _rope(ki, pos), wi)


def _rel_bias(rel_bias, dist):
    return rel_bias[:, jnp.clip(dist, -MAX_REL, MAX_REL) + MAX_REL].astype(jnp.float32)


def _band_attn_prompt(q, k, v, rel_bias):
    b, s, h, d = q.shape
    nc = s // CHUNK
    chunked = lambda t: t.reshape(b, nc, CHUNK, h, d)
    pad = ((0, 0), (LEFT_CHUNKS, 0), (0, 0), (0, 0), (0, 0))
    kc = jnp.pad(chunked(k), pad)
    vc = jnp.pad(chunked(v), pad)
    k_band = jnp.concatenate([kc[:, j:j + nc] for j in range(BAND_CHUNKS)], axis=2)
    v_band = jnp.concatenate([vc[:, j:j + nc] for j in range(BAND_CHUNKS)], axis=2)
    qi = jnp.arange(CHUNK)
    kj = jnp.arange(BAND_CHUNKS * CHUNK)
    dist = LEFT_CHUNKS * CHUNK + qi[:, None] - kj[None, :]
    bias = _rel_bias(rel_bias, dist)
    valid = (jnp.arange(nc)[:, None] + kj[None, :] // CHUNK) >= LEFT_CHUNKS
    sc = jnp.einsum('bcqhd,bckhd->bchqk', chunked(q), k_band).astype(jnp.float32) * (d ** -0.5)
    sc = jnp.where(valid[None, :, None, None, :], sc + bias[None, None], -jnp.inf)
    p = jax.nn.softmax(sc, axis=-1).astype(v.dtype)
    o = jnp.einsum('bchqk,bckhd->bcqhd', p, v_band)
    return o.reshape(b, s, h * d)


def _band_attn_sample(q, k_new, v_new, k_cache, v_cache, rel_bias):
    b, n, h, d = q.shape
    L = k_cache.shape[1]
    k = jnp.concatenate([k_cache, k_new], axis=1)
    v = jnp.concatenate([v_cache, v_new], axis=1)
    dist = (L + jnp.arange(n))[:, None] - jnp.arange(L + n)[None, :]
    sc = jnp.einsum('bqhd,bkhd->bhqk', q, k).astype(jnp.float32) * (d ** -0.5)
    sc = sc + _rel_bias(rel_bias, dist)[None]
    p = jax.nn.softmax(sc, axis=-1).astype(v.dtype)
    o = jnp.einsum('bhqk,bkhd->bqhd', p, v)
    return o.reshape(b, n, h * d)


def _dsa_attend(q, qi, wi, k, v, ki, adm, k_sel):
    isc = jax.nn.relu(jnp.einsum('bqhd,bsd->bqhs', qi, ki).astype(jnp.float32) * (IDX_DIM ** -0.5))
    isc = jnp.einsum('bqhs,bqh->bqs', isc, wi.astype(jnp.float32) * (N_IDX_HEADS ** -0.5))
    isc = jnp.where(adm, isc, -jnp.inf)
    _, idx = lax.top_k(isc, k_sel)
    valid = jnp.take_along_axis(jnp.broadcast_to(adm, isc.shape), idx, axis=-1)
    kg = jax.vmap(lambda kk, ii: kk[ii])(k, idx)
    vg = jax.vmap(lambda vv, ii: vv[ii])(v, idx)
    sc = jnp.einsum('bqhd,bqkhd->bqhk', q, kg).astype(jnp.float32) * (q.shape[-1] ** -0.5)
    sc = jnp.where(valid[:, :, None, :], sc, -jnp.inf)
    p = jax.nn.softmax(sc, axis=-1).astype(v.dtype)
    return jnp.einsum('bqhk,bqkhd->bqhd', p, vg)


def _dsa_prompt(q, k, v, qi, ki, wi, k_sel):
    b, s, h, d = q.shape
    nb = s // Q_BLOCK
    to_blocks = lambda t: jnp.moveaxis(t.reshape((b, nb, Q_BLOCK) + t.shape[2:]), 1, 0)
    key_chunk = jnp.arange(s) // CHUNK

    def block(args):
        qb, qib, wib, bi = args
        qpos = bi * Q_BLOCK + jnp.arange(Q_BLOCK)
        adm = key_chunk[None, :] <= (qpos // CHUNK)[:, None]
        return _dsa_attend(qb, qib, wib, k, v, ki, adm[None], k_sel)

    o = lax.map(block, (to_blocks(q), to_blocks(qi), to_blocks(wi), jnp.arange(nb)))
    return jnp.moveaxis(o, 0, 1).reshape(b, s, h * d)


def _dsa_sample(q, k_new, v_new, qi, ki_new, wi, k_cache, v_cache, ki_cache, k_sel):
    n = q.shape[1]
    k = jnp.concatenate([k_cache, k_new], axis=1)
    v = jnp.concatenate([v_cache, v_new], axis=1)
    ki = jnp.concatenate([ki_cache, ki_new], axis=1)
    adm = jnp.ones((1, n, k.shape[1]), dtype=bool)
    o = _dsa_attend(q, qi, wi, k, v, ki, adm, k_sel)
    return o.reshape(q.shape[0], n, -1)


def setup_inputs(seed: int = 0) -> dict:
    key = jax.random.key(seed)
    ks = jax.random.split(key, 20)
    f32 = jnp.float32
    nrm = lambda k, shape, scale: jax.random.normal(k, shape, f32) * scale
    a_len = min(LEFT_CHUNKS * CHUNK, PAST_LEN)
    return {
        "x_prompt": nrm(ks[0], (BATCH, SEQ, D_MODEL), 1.0),
        "x_sample": nrm(ks[1], (DEC_BATCH, DEC_SEQ, D_MODEL), 1.0),
        "cache_a_k": nrm(ks[2], (DEPTH, DEC_BATCH, a_len, N_HEADS_A, HEAD_DIM), 1.0),
        "cache_a_v": nrm(ks[3], (DEPTH, DEC_BATCH, a_len, N_HEADS_A, HEAD_DIM), 1.0),
        "cache_b_k": nrm(ks[4], (DEPTH, DEC_BATCH, PAST_LEN, N_HEADS_B, HEAD_DIM), 1.0),
        "cache_b_v": nrm(ks[5], (DEPTH, DEC_BATCH, PAST_LEN, N_HEADS_B, HEAD_DIM), 1.0),
        "cache_idx_k": nrm(ks[6], (DEPTH, DEC_BATCH, PAST_LEN, IDX_DIM), 1.0),
        "ffn1_norm": 1.0 + nrm(ks[7], (DEPTH, D_MODEL), 0.01),
        "ffn1_w_gate": nrm(ks[8], (DEPTH, D_MODEL, D_FF), D_MODEL ** -0.5),
        "ffn1_w_up": nrm(ks[9], (DEPTH, D_MODEL, D_FF), D_MODEL ** -0.5),
        "ffn1_w_down": nrm(ks[10], (DEPTH, D_FF, D_MODEL), D_FF ** -0.5),
        "mix_norm": 1.0 + nrm(ks[11], (DEPTH, D_MODEL), 0.01),
        "w_in": nrm(ks[12], (DEPTH, D_MODEL, D_IN), D_MODEL ** -0.5),
        "rel_bias": nrm(ks[13], (DEPTH, N_HEADS_A, N_REL), 0.1),
        "w_out": nrm(ks[14], (DEPTH, D_MIX, D_MODEL), D_MIX ** -0.5),
        "ffn2_norm": 1.0 + nrm(ks[15], (DEPTH, D_MODEL), 0.01),
        "ffn2_w_gate": nrm(ks[16], (DEPTH, D_MODEL, D_FF), D_MODEL ** -0.5),
        "ffn2_w_up": nrm(ks[17], (DEPTH, D_MODEL, D_FF), D_MODEL ** -0.5),
        "ffn2_w_down": nrm(ks[18], (DEPTH, D_FF, D_MODEL), D_FF ** -0.5),
        "final_norm": 1.0 + nrm(ks[19], (D_MODEL,), 0.01),
    }


def reference(x_prompt, x_sample, cache_a_k, cache_a_v, cache_b_k, cache_b_v, cache_idx_k,
              ffn1_norm, ffn1_w_gate, ffn1_w_up, ffn1_w_down, mix_norm, w_in, rel_bias, w_out,
              ffn2_norm, ffn2_w_gate, ffn2_w_up, ffn2_w_down, final_norm):
    s = x_prompt.shape[1]
    n = x_sample.shape[1]
    past = cache_b_k.shape[2]
    pos_p = jnp.arange(s)
    pos_s = past + jnp.arange(n)
    k_sel_p = min(TOPK_MAX, s // 4)
    k_sel_s = min(TOPK_MAX, (past + n) // 4)
    a_keep = min(LEFT_CHUNKS * CHUNK, s)

    yp, ys = x_prompt, x_sample
    akp, avp, bkp, bvp, ikp = [], [], [], [], []
    aks, avs, bks, bvs, iks = [], [], [], [], []
    for l in range(DEPTH):
        yp = _macaron_ffn(yp, ffn1_norm[l], ffn1_w_gate[l], ffn1_w_up[l], ffn1_w_down[l])
        qa, ka, va, qb, kb, vb, qi, ki, wi = _mixer_inputs(yp, pos_p, mix_norm[l], w_in[l])
        o_a = _band_attn_prompt(qa, ka, va, rel_bias[l])
        o_b = _dsa_prompt(qb, kb, vb, qi, ki, wi, k_sel_p)
        yp = yp + jnp.concatenate([o_a, o_b], axis=-1) @ w_out[l]
        yp = _macaron_ffn(yp, ffn2_norm[l], ffn2_w_gate[l], ffn2_w_up[l], ffn2_w_down[l])
        akp.append(ka[:, s - a_keep:])
        avp.append(va[:, s - a_keep:])
        bkp.append(kb)
        bvp.append(vb)
        ikp.append(ki)

        ys = _macaron_ffn(ys, ffn1_norm[l], ffn1_w_gate[l], ffn1_w_up[l], ffn1_w_down[l])
        qa, ka, va, qb, kb, vb, qi, ki, wi = _mixer_inputs(ys, pos_s, mix_norm[l], w_in[l])
        o_a = _band_attn_sample(qa, ka, va, cache_a_k[l], cache_a_v[l], rel_bias[l])
        o_b = _dsa_sample(qb, kb, vb, qi, ki, wi, cache_b_k[l], cache_b_v[l], cache_idx_k[l], k_sel_s)
        ys = ys + jnp.concatenate([o_a, o_b], axis=-1) @ w_out[l]
        ys = _macaron_ffn(ys, ffn2_norm[l], ffn2_w_gate[l], ffn2_w_up[l], ffn2_w_down[l])
        aks.append(ka)
        avs.append(va)
        bks.append(kb)
        bvs.append(vb)
        iks.append(ki)

    yp = _rmsnorm(yp, final_norm)
    ys = _rmsnorm(ys, final_norm)
    return (yp, ys,
            jnp.stack(akp), jnp.stack(avp), jnp.stack(bkp), jnp.stack(bvp), jnp.stack(ikp),
            jnp.stack(aks), jnp.stack(avs), jnp.stack(bks), jnp.stack(bvs), jnp.stack(iks))
```

```python
import functools

import numpy as np
import jax
import jax.numpy as jnp
from jax import lax
from jax.experimental import pallas as pl
from jax.experimental.pallas import tpu as pltpu

CHUNK = 64
LEFT_CHUNKS = 8
HEAD_DIM = 64
N_HEADS = 8
D_HEADS = N_HEADS * HEAD_DIM
N_IDX_HEADS = 8
IDX_DIM = 32
TOPK_MAX = 256
MAX_REL = 128
ROPE_THETA = 10000.0
EPS = 1e-6

LANES = 128
VMEM_LIMIT_BYTES = 56 * 1024 * 1024

TOKEN_TILE = 256
FF_SPLIT = 2
BAND_Q_TILE = 256
DSA_TILE = 256
SEARCH_ROWS = 128

NEG = -1e30
F32_MAX = float(np.finfo(np.float32).max)

f32 = jnp.float32
bf16 = jnp.bfloat16
i32 = jnp.int32

_NT = (((1,), (1,)), ((), ()))


def _dot(a, b):
    return jnp.dot(a, b, preferred_element_type=f32)


def _dot_nt(a, b):
    return lax.dot_general(a, b, _NT, preferred_element_type=f32)


def _rms(x, g):
    return x * lax.rsqrt(jnp.mean(x * x, axis=-1, keepdims=True) + EPS) * g


def _swiglu(h, wg_ref, wu_ref, wd_ref):
    d_ff = wg_ref.shape[1]
    fc = d_ff // FF_SPLIT
    acc = None
    for c0 in range(0, d_ff, fc):
        g = _dot(h, wg_ref[:, c0:c0 + fc])
        u = _dot(h, wu_ref[:, c0:c0 + fc])
        a = (g * jax.nn.sigmoid(g) * u).astype(bf16)
        part = _dot(a, wd_ref[c0:c0 + fc, :])
        acc = part if acc is None else acc + part
    return acc


def _rope_block(x, cos, sin_signed, half):
    lane = lax.broadcasted_iota(i32, x.shape, 1)
    first = (lane % (2 * half)) < half
    partner = jnp.where(first, pltpu.roll(x, LANES - half, 1), pltpu.roll(x, half, 1))
    return x * cos + partner * sin_signed


def _ffn_inproj_body(x_ref, g1_ref, wg_ref, wu_ref, wd_ref, gm_ref, win_ref,
                     cb_ref, sb_ref, ci_ref, si_ref, ct_ref, st_ref,
                     x1_ref, qa_ref, ka_ref, va_ref, qb_ref, kb_ref, vb_ref,
                     ka32_ref, va32_ref, kb32_ref, vb32_ref, qi_ref, tail_ref, kirep_ref):
    x = x_ref[...]
    x1 = x + 0.5 * _swiglu(_rms(x, g1_ref[...]).astype(bf16), wg_ref, wu_ref, wd_ref)
    x1_ref[...] = x1
    hm = _rms(x1, gm_ref[...]).astype(bf16)

    def proj(c0, width):
        return _dot(hm, win_ref[:, c0:c0 + width])

    def rope_heads(p, cos_ref, sin_ref, half):
        cos, sin = cos_ref[...], sin_ref[...]
        blocks = [_rope_block(p[:, c:c + LANES], cos, sin, half) for c in range(0, p.shape[1], LANES)]
        return jnp.concatenate(blocks, axis=1)

    scale = HEAD_DIM ** -0.5
    qa = proj(0, D_HEADS)
    qa_ref[...] = (qa * scale).astype(bf16)
    ka = proj(D_HEADS, D_HEADS)
    ka32_ref[...] = ka
    ka_ref[...] = ka.astype(bf16)
    va = proj(2 * D_HEADS, D_HEADS)
    va32_ref[...] = va
    va_ref[...] = va.astype(bf16)
    qb = rope_heads(proj(3 * D_HEADS, D_HEADS), cb_ref, sb_ref, HEAD_DIM // 2)
    qb_ref[...] = (qb * scale).astype(bf16)
    kb = rope_heads(proj(4 * D_HEADS, D_HEADS), cb_ref, sb_ref, HEAD_DIM // 2)
    kb32_ref[...] = kb
    kb_ref[...] = kb.astype(bf16)
    vb = proj(5 * D_HEADS, D_HEADS)
    vb32_ref[...] = vb
    vb_ref[...] = vb.astype(bf16)
    qi = rope_heads(proj(6 * D_HEADS, N_IDX_HEADS * IDX_DIM), ci_ref, si_ref, IDX_DIM // 2)
    qi_ref[...] = qi.astype(bf16)
    tail = _rope_block(proj(6 * D_HEADS + N_IDX_HEADS * IDX_DIM, LANES), ct_ref[...], st_ref[...], IDX_DIM // 2)
    tail_ref[...] = tail
    lane = lax.broadcasted_iota(i32, tail.shape, 1)
    k0 = jnp.where(lane < IDX_DIM, tail, 0.0)
    rep = k0 + pltpu.roll(k0, IDX_DIM, 1) + pltpu.roll(k0, 2 * IDX_DIM, 1) + pltpu.roll(k0, 3 * IDX_DIM, 1)
    kirep_ref[...] = rep.astype(bf16)


def _const_spec(shape):
    nd = len(shape)
    return pl.BlockSpec(shape, lambda *_: (0,) * nd, pipeline_mode=pl.Buffered(1))


def _ffn_inproj(x, g1, wg, wu, wd, gm, win, tables, n_tab_blocks):
    n, d = x.shape
    tm = TOKEN_TILE
    assert n % tm == 0
    row = lambda w: pl.BlockSpec((tm, w), lambda i: (i, 0))
    tab = pl.BlockSpec((tm, LANES), lambda i: (i % n_tab_blocks, 0))
    out_shapes = (
        [jax.ShapeDtypeStruct((n, d), f32)]
        + [jax.ShapeDtypeStruct((n, D_HEADS), bf16)] * 6
        + [jax.ShapeDtypeStruct((n, D_HEADS), f32)] * 4
        + [jax.ShapeDtypeStruct((n, N_IDX_HEADS * IDX_DIM), bf16),
           jax.ShapeDtypeStruct((n, LANES), f32),
           jax.ShapeDtypeStruct((n, LANES), bf16)])
    out_specs = ([row(d)] + [row(D_HEADS)] * 10 + [row(N_IDX_HEADS * IDX_DIM), row(LANES), row(LANES)])
    return pl.pallas_call(
        _ffn_inproj_body,
        grid=(n // tm,),
        in_specs=[row(d), _const_spec(g1.shape), _const_spec(wg.shape), _const_spec(wu.shape),
                  _const_spec(wd.shape), _const_spec(gm.shape), _const_spec(win.shape)] + [tab] * 6,
        out_specs=out_specs,
        out_shape=out_shapes,
        compiler_params=pltpu.CompilerParams(dimension_semantics=("arbitrary",),
                                             vmem_limit_bytes=VMEM_LIMIT_BYTES),
        name="ffn_inproj",
    )(x, g1, wg, wu, wd, gm, win, *tables)


def _out_ffn_body(x1_ref, oa_ref, ob_ref, wo_ref, g2_ref, wg_ref, wu_ref, wd_ref, gf_ref, y_ref, *, final):
    x2 = x1_ref[...] + _dot(oa_ref[...], wo_ref[:D_HEADS, :]) + _dot(ob_ref[...], wo_ref[D_HEADS:, :])
    x3 = x2 + 0.5 * _swiglu(_rms(x2, g2_ref[...]).astype(bf16), wg_ref, wu_ref, wd_ref)
    y_ref[...] = _rms(x3, gf_ref[...]) if final else x3


def _out_ffn(x1, oa, ob, wo, g2, wg, wu, wd, gf, final):
    n, d = x1.shape
    tm = TOKEN_TILE
    row = lambda w: pl.BlockSpec((tm, w), lambda i: (i, 0))
    return pl.pallas_call(
        functools.partial(_out_ffn_body, final=final),
        grid=(n // tm,),
        in_specs=[row(d), row(D_HEADS), row(D_HEADS)] + [_const_spec(a.shape) for a in (wo, g2, wg, wu, wd, gf)],
        out_specs=row(d),
        out_shape=jax.ShapeDtypeStruct((n, d), f32),
        compiler_params=pltpu.CompilerParams(dimension_semantics=("arbitrary",),
                                             vmem_limit_bytes=VMEM_LIMIT_BYTES),
        name="out_ffn",
    )(x1, oa, ob, wo, g2, wg, wu, wd, gf)


def _head_lane_mask(shape, parity):
    lane = lax.broadcasted_iota(i32, shape, 1)
    return (lane < HEAD_DIM) if parity == 0 else (lane >= HEAD_DIM)


def _band_prompt_body(q_ref, k_ref, v_ref, bias_ref, o_ref, *, tq, win):
    t = pl.program_id(1)
    r0 = pl.multiple_of(t * tq, tq)
    q = q_ref[...]
    col = lax.broadcasted_iota(i32, (tq, win), 1)
    started = col >= (LEFT_CHUNKS * CHUNK - t * tq)
    for p in range(N_HEADS // 2):
        sl = slice(p * LANES, (p + 1) * LANES)
        qp = q[:, sl]
        kp = k_ref[pl.ds(r0, win), sl]
        vp = v_ref[pl.ds(r0, win), sl]
        outs = []
        for e in range(2):
            qm = jnp.where(_head_lane_mask(qp.shape, e), qp, jnp.zeros_like(qp))
            s = _dot_nt(qm, kp) + bias_ref[2 * p + e]
            s = jnp.where(started, s, -jnp.inf)
            m = jnp.max(s, axis=1, keepdims=True)
            pe = jnp.exp(s - m)
            l = jnp.sum(pe, axis=1, keepdims=True)
            outs.append(_dot(pe.astype(bf16), vp) / l)
        o_ref[:, sl] = jnp.where(_head_lane_mask(outs[0].shape, 0), outs[0], outs[1]).astype(bf16)


def _band_prompt(qa, ka_pad, va_pad, bias_tab, b, s):
    tq = BAND_Q_TILE
    win = tq + LEFT_CHUNKS * CHUNK
    nt = s // tq
    kv_spec = pl.BlockSpec((None, s + LEFT_CHUNKS * CHUNK, D_HEADS), lambda bi, t: (bi, 0, 0),
                           pipeline_mode=pl.Buffered(1))
    return pl.pallas_call(
        functools.partial(_band_prompt_body, tq=tq, win=win),
        grid=(b, nt),
        in_specs=[pl.BlockSpec((tq, D_HEADS), lambda bi, t: (bi * nt + t, 0)), kv_spec, kv_spec,
                  _const_spec(bias_tab.shape)],
        out_specs=pl.BlockSpec((tq, D_HEADS), lambda bi, t: (bi * nt + t, 0)),
        out_shape=jax.ShapeDtypeStruct((b * s, D_HEADS), bf16),
        compiler_params=pltpu.CompilerParams(dimension_semantics=("arbitrary", "arbitrary"),
                                             vmem_limit_bytes=VMEM_LIMIT_BYTES),
        name="band_prompt",
    )(qa, ka_pad, va_pad, bias_tab)


def _ordered_bits_to_f32(u):
    bits = jnp.where(u < 0, u & jnp.int32(0x7FFFFFFF), ~u)
    return lax.bitcast_convert_type(bits, f32)


def _select_to_bias(isc_ref, jstar_ref, nkt, k_sel, n_adm, *, rows, rb, tk):
    n_chunks = tk // LANES
    idx_bits = max(1, int(np.ceil(np.log2(isc_ref.shape[0] * tk + 1))))
    lane = lax.broadcasted_iota(i32, (rb, LANES), 1)
    big = jnp.int32(2 ** 30)

    for r0 in range(0, rows, rb):
        rs = slice(r0, r0 + rb)
        nadm = n_adm[rs]

        def count(pred):
            def body(kt, acc):
                for c in range(n_chunks):
                    tile = isc_ref[kt, rs, c * LANES:(c + 1) * LANES]
                    acc = acc + jnp.where(pred(tile, kt * tk + c * LANES + lane), 1.0, 0.0)
                return acc
            acc = lax.fori_loop(0, nkt, body, jnp.zeros((rb, LANES), f32))
            return jnp.sum(acc, axis=1, keepdims=True)

        def value_bit(i, prefix):
            cand = prefix | lax.shift_left(jnp.int32(1), 31 - i)
            thr = jnp.broadcast_to(_ordered_bits_to_f32(cand), (rb, LANES))
            cnt = count(lambda tile, idx: tile >= thr)
            return jnp.where(cnt >= k_sel, cand, prefix)

        prefix = lax.fori_loop(0, 32, value_bit, jnp.zeros((rb, 1), i32))
        select_all = nadm <= k_sel
        tau = jnp.where(select_all, -F32_MAX, _ordered_bits_to_f32(prefix))
        tau_b = jnp.broadcast_to(tau, (rb, LANES))

        n_gt = count(lambda tile, idx: tile > tau_b)
        n_eq = count(lambda tile, idx: tile == tau_b)
        need = k_sel - n_gt
        excess = jnp.logical_and(n_eq > need, jnp.logical_not(select_all))
        jstar_ref[...] = jnp.full((rb, 1), big, i32)

        @pl.when(jnp.max(jnp.where(excess, 1.0, 0.0)) > 0.0)
        def _():
            def index_bit(i, p):
                cand = p + lax.shift_left(jnp.int32(1), idx_bits - 1 - i)
                cand_b = jnp.broadcast_to(cand, (rb, LANES))
                cnt = count(lambda tile, idx: jnp.logical_and(tile == tau_b, idx < cand_b))
                return jnp.where(cnt < need, cand, p)
            p = lax.fori_loop(0, idx_bits, index_bit, jnp.zeros((rb, 1), i32))
            jstar_ref[...] = jnp.where(excess, p, big)

        j_b = jnp.broadcast_to(jstar_ref[...], (rb, LANES))

        def to_bias(kt, carry):
            for c in range(n_chunks):
                cs = slice(c * LANES, (c + 1) * LANES)
                tile = isc_ref[kt, rs, cs]
                idx = kt * tk + c * LANES + lane
                sel = jnp.logical_or(tile > tau_b, jnp.logical_and(tile == tau_b, idx <= j_b))
                isc_ref[kt, rs, cs] = jnp.where(sel, 0.0, NEG)
            return carry
        lax.fori_loop(0, nkt, to_bias, 0)


def _dsa_prompt_body(qi_ref, tail_ref, qb_ref, kirep_ref, kb_ref, vb_ref, o_ref,
                     isc_ref, qim_ref, qbm_ref, wb_ref, m_ref, l_ref, acc_ref, oe_ref, jstar_ref,
                     *, tq, k_sel):
    t = pl.program_id(1)
    nkt = t + 1
    lane = lax.broadcasted_iota(i32, (tq, LANES), 1)

    qi = qi_ref[...]
    per_group = LANES // IDX_DIM
    for h in range(N_IDX_HEADS):
        g, e = divmod(h, per_group)
        blk = qi[:, g * LANES:(g + 1) * LANES]
        keep = jnp.logical_and(lane >= e * IDX_DIM, lane < (e + 1) * IDX_DIM)
        qim_ref[h] = jnp.where(keep, blk, jnp.zeros_like(blk))
    qb = qb_ref[...]
    for h in range(N_HEADS):
        blk = qb[:, (h // 2) * LANES:(h // 2 + 1) * LANES]
        qbm_ref[h] = jnp.where(_head_lane_mask(blk.shape, h % 2), blk, jnp.zeros_like(blk))
    w = tail_ref[...] * (IDX_DIM ** -0.5 * N_IDX_HEADS ** -0.5)
    for h in range(N_IDX_HEADS):
        wb_ref[h] = jnp.broadcast_to(w[:, IDX_DIM + h:IDX_DIM + h + 1], (tq, LANES))

    def index_tile(kt, carry):
        koff = pl.multiple_of(kt * tq, tq)
        kr = kirep_ref[pl.ds(koff, tq), :]
        acc = jnp.zeros((tq, tq), f32)
        for h in range(N_IDX_HEADS):
            s = _dot_nt(qim_ref[h], kr)
            wh = wb_ref[h]
            acc = acc + jnp.maximum(s, 0.0) * jnp.concatenate([wh] * (tq // LANES), axis=1)
        isc_ref[kt] = acc
        return carry
    lax.fori_loop(0, nkt, index_tile, 0)
    rc = lax.broadcasted_iota(i32, (tq, tq), 0) // CHUNK
    cc = lax.broadcasted_iota(i32, (tq, tq), 1) // CHUNK
    isc_ref[t] = jnp.where(cc > rc, -jnp.inf, isc_ref[t])

    n_adm = t * tq + (lax.broadcasted_iota(i32, (tq, 1), 0) // CHUNK + 1) * CHUNK
    _select_to_bias(isc_ref, jstar_ref, nkt, k_sel, n_adm, rows=tq, rb=min(SEARCH_ROWS, tq), tk=tq)

    for h in range(N_HEADS):
        sl = slice((h // 2) * LANES, (h // 2 + 1) * LANES)
        m_ref[...] = jnp.full(m_ref.shape, NEG, f32)
        l_ref[...] = jnp.zeros(l_ref.shape, f32)
        acc_ref[...] = jnp.zeros(acc_ref.shape, f32)

        def kv_tile(kt, carry):
            koff = pl.multiple_of(kt * tq, tq)
            s = _dot_nt(qbm_ref[h], kb_ref[pl.ds(koff, tq), sl]) + isc_ref[kt]
            m_prev = m_ref[...]
            m_new = jnp.maximum(m_prev, jnp.max(s, axis=1, keepdims=True))
            alpha = jnp.exp(m_prev - m_new)
            pe = jnp.exp(s - m_new)
            l_ref[...] = alpha * l_ref[...] + jnp.sum(pe, axis=1, keepdims=True)
            acc_ref[...] = alpha * acc_ref[...] + _dot(pe.astype(bf16), vb_ref[pl.ds(koff, tq), sl])
            m_ref[...] = m_new
            return carry
        lax.fori_loop(0, nkt, kv_tile, 0)
        o = acc_ref[...] / l_ref[...]
        if h % 2 == 0:
            oe_ref[...] = o
        else:
            o_ref[:, sl] = jnp.where(lane < HEAD_DIM, oe_ref[...], o).astype(bf16)


def _dsa_prompt(qi, tail, qb, kirep, kb, vb, b, s, k_sel):
    tq = DSA_TILE
    assert s % tq == 0 and tq % CHUNK == 0
    nt = s // tq
    tile = lambda w: pl.BlockSpec((tq, w), lambda bi, t: (bi * nt + t, 0))
    resident = lambda w: pl.BlockSpec((None, s, w), lambda bi, t: (bi, 0, 0), pipeline_mode=pl.Buffered(1))
    return pl.pallas_call(
        functools.partial(_dsa_prompt_body, tq=tq, k_sel=k_sel),
        grid=(b, nt),
        in_specs=[tile(N_IDX_HEADS * IDX_DIM), tile(LANES), tile(D_HEADS),
                  resident(LANES), resident(D_HEADS), resident(D_HEADS)],
        out_specs=tile(D_HEADS),
        out_shape=jax.ShapeDtypeStruct((b * s, D_HEADS), bf16),
        scratch_shapes=[
            pltpu.VMEM((nt, tq, tq), f32),
            pltpu.VMEM((N_IDX_HEADS, tq, LANES), bf16),
            pltpu.VMEM((N_HEADS, tq, LANES), bf16),
            pltpu.VMEM((N_IDX_HEADS, tq, LANES), f32),
            pltpu.VMEM((tq, 1), f32), pltpu.VMEM((tq, 1), f32),
            pltpu.VMEM((tq, LANES), f32), pltpu.VMEM((tq, LANES), f32),
            pltpu.VMEM((min(SEARCH_ROWS, tq), 1), i32),
        ],
        compiler_params=pltpu.CompilerParams(dimension_semantics=("arbitrary", "arbitrary"),
                                             vmem_limit_bytes=VMEM_LIMIT_BYTES),
        name="dsa_prompt",
    )(qi, tail, qb, kirep.reshape(b, s, LANES), kb.reshape(b, s, D_HEADS), vb.reshape(b, s, D_HEADS))


def _sample_body(qa_ref, ka_ref, va_ref, qb_ref, kb_ref, vb_ref, qi_ref, tail_ref,
                 cak_ref, cav_ref, cbk_ref, cbv_ref, cik_ref, bias_a_ref,
                 oa_ref, ob_ref, isc_ref, jstar_ref, *, n, k_sel):
    rows = N_HEADS * n
    a_len = cak_ref.shape[0]
    past = cbk_ref.shape[0]

    def expand(a):
        return jnp.concatenate([a] * N_HEADS, axis=0)

    def head_rows(shape, group):
        return (lax.broadcasted_iota(i32, shape, 0) // n) == (lax.broadcasted_iota(i32, shape, 1) // group)

    def pad_rows(a):
        return jnp.concatenate([a, jnp.zeros((LANES - n, a.shape[1]), a.dtype)], axis=0)

    def block_diag_q(q_ref):
        q = expand(q_ref[...].astype(f32))
        return jnp.where(head_rows(q.shape, HEAD_DIM), q, 0.0).astype(bf16)

    def attend(qbd, k_cache, k_new, v_cache, v_new, bias):
        s = jnp.concatenate([_dot_nt(qbd, k_cache), _dot_nt(qbd, pad_rows(k_new))], axis=1) + bias
        m = jnp.max(s, axis=1, keepdims=True)
        pe = jnp.exp(s - m)
        l = jnp.sum(pe, axis=1, keepdims=True)
        pb = pe.astype(bf16)
        lc = k_cache.shape[0]
        o = (_dot(pb[:, :lc], v_cache) + _dot(pb[:, lc:], pad_rows(v_new))) / l
        o = jnp.where(head_rows(o.shape, HEAD_DIM), o, 0.0)
        out = o[0:n]
        for h in range(1, N_HEADS):
            out = out + o[h * n:(h + 1) * n]
        return out

    oa_ref[...] = attend(block_diag_q(qa_ref), cak_ref[...].astype(bf16), ka_ref[...],
                         cav_ref[...].astype(bf16), va_ref[...], bias_a_ref[...]).astype(bf16)

    tail = tail_ref[...]
    qi = expand(qi_ref[...].astype(f32))
    qi = jnp.where(head_rows(qi.shape, IDX_DIM), qi, 0.0)
    fold = qi[:, :LANES] + qi[:, LANES:]
    fold = fold + pltpu.roll(fold, 2 * IDX_DIM, 1)
    fold = fold + pltpu.roll(fold, IDX_DIM, 1)
    qi2 = fold[:, :IDX_DIM].astype(bf16)
    t8 = expand(tail)
    lane = lax.broadcasted_iota(i32, t8.shape, 1)
    hrow = lax.broadcasted_iota(i32, t8.shape, 0) // n
    wcol = jnp.sum(jnp.where(lane == IDX_DIM + hrow, t8, 0.0), axis=1, keepdims=True)
    wcol = wcol * (IDX_DIM ** -0.5 * N_IDX_HEADS ** -0.5)

    def head_sum(r):
        out = r[0:n]
        for h in range(1, N_IDX_HEADS):
            out = out + r[h * n:(h + 1) * n]
        return out

    isc_c = head_sum(jnp.maximum(_dot_nt(qi2, cik_ref[...].astype(bf16)), 0.0) * wcol)
    ki_new = pad_rows(tail[:, :IDX_DIM].astype(bf16))
    isc_n = head_sum(jnp.maximum(_dot_nt(qi2, ki_new), 0.0) * wcol)
    lane_n = lax.broadcasted_iota(i32, isc_n.shape, 1)
    isc_ref[0, :, :past] = isc_c
    isc_ref[0, :, past:] = jnp.where(lane_n < n, isc_n, -jnp.inf)
    n_adm = jnp.full((n, 1), past + n, i32)
    _select_to_bias(isc_ref, jstar_ref, 1, k_sel, n_adm, rows=n, rb=n, tk=past + LANES)

    ob_ref[...] = attend(block_diag_q(qb_ref), cbk_ref[...].astype(bf16), kb_ref[...],
                         cbv_ref[...].astype(bf16), vb_ref[...], expand(isc_ref[0])).astype(bf16)


def _sample_mixers(qa, ka, va, qb, kb, vb, qi, tail, cak, cav, cbk, cbv, cik, bias_a, db, n, k_sel):
    a_len, past = cak.shape[1], cbk.shape[1]
    assert past % LANES == 0 and a_len % LANES == 0 and n % 16 == 0 and N_HEADS * n == LANES
    tile = lambda w: pl.BlockSpec((n, w), lambda bi: (bi, 0))
    cache = lambda rows, w: pl.BlockSpec((None, rows, w), lambda bi: (bi, 0, 0))
    return pl.pallas_call(
        functools.partial(_sample_body, n=n, k_sel=k_sel),
        grid=(db,),
        in_specs=[tile(D_HEADS)] * 6 + [tile(N_IDX_HEADS * IDX_DIM), tile(LANES),
                  cache(a_len, D_HEADS), cache(a_len, D_HEADS), cache(past, D_HEADS), cache(past, D_HEADS),
                  cache(past, IDX_DIM), _const_spec(bias_a.shape)],
        out_specs=[tile(D_HEADS), tile(D_HEADS)],
        out_shape=[jax.ShapeDtypeStruct((db * n, D_HEADS), bf16)] * 2,
        scratch_shapes=[pltpu.VMEM((1, n, past + LANES), f32), pltpu.VMEM((n, 1), i32)],
        compiler_params=pltpu.CompilerParams(dimension_semantics=("arbitrary",),
                                             vmem_limit_bytes=VMEM_LIMIT_BYTES),
        name="sample_mixers",
    )(qa, ka, va, qb, kb, vb, qi, tail, cak, cav, cbk, cbv, cik, bias_a)


def _rope_tables(pos, half, live_lanes):
    inv = ROPE_THETA ** (-jnp.arange(half, dtype=f32) / half)
    ang = pos.astype(f32)[:, None] * inv[None, :]
    lane = np.arange(LANES)
    cos = jnp.cos(ang)[:, lane % half]
    sin = jnp.sin(ang)[:, lane % half]
    first = jnp.asarray((lane % (2 * half)) < half)
    live = jnp.asarray(lane < live_lanes)
    cos = jnp.where(live[None, :], cos, 1.0)
    sin = jnp.where(live[None, :], jnp.where(first[None, :], -sin, sin), 0.0)
    return cos, sin


def _all_rope_tables(pos):
    cb, sb = _rope_tables(pos, HEAD_DIM // 2, LANES)
    ci, si = _rope_tables(pos, IDX_DIM // 2, LANES)
    ct, st = _rope_tables(pos, IDX_DIM // 2, IDX_DIM)
    return (cb, sb, ci, si, ct, st)


def _rel_bias_lookup(rel_bias, dist):
    return rel_bias[:, np.clip(dist, -MAX_REL, MAX_REL) + MAX_REL].astype(f32)


def _band_prompt_bias(rel_bias):
    tq = BAND_Q_TILE
    win = tq + LEFT_CHUNKS * CHUNK
    qi = np.arange(tq)[:, None]
    kj = np.arange(win)[None, :]
    dist = LEFT_CHUNKS * CHUNK + qi - kj
    in_band = np.logical_and(kj // CHUNK >= qi // CHUNK, kj // CHUNK <= qi // CHUNK + LEFT_CHUNKS)
    return jnp.where(jnp.asarray(in_band)[None], _rel_bias_lookup(rel_bias, dist), -jnp.inf)


def _band_sample_bias(rel_bias, n, a_len):
    qi = np.arange(n)[:, None]
    kj = np.arange(a_len + LANES)[None, :]
    dist = a_len + qi - kj
    bias = jnp.where(jnp.asarray(kj < a_len + n)[None], _rel_bias_lookup(rel_bias, dist), NEG)
    return bias.reshape(N_HEADS * n, a_len + LANES)


def kernel(x_prompt, x_sample, cache_a_k, cache_a_v, cache_b_k, cache_b_v, cache_idx_k, ffn1_norm, ffn1_w_gate, ffn1_w_up, ffn1_w_down, mix_norm, w_in, rel_bias, w_out, ffn2_norm, ffn2_w_gate, ffn2_w_up, ffn2_w_down, final_norm):
    b, s, d = x_prompt.shape
    db, n, _ = x_sample.shape
    depth = w_in.shape[0]
    past = cache_b_k.shape[2]
    a_len = cache_a_k.shape[2]
    k_sel_p = min(TOPK_MAX, s // 4)
    k_sel_s = min(TOPK_MAX, (past + n) // 4)
    a_keep = min(LEFT_CHUNKS * CHUNK, s)
    d_in = w_in.shape[2]
    assert d_in == 6 * D_HEADS + N_IDX_HEADS * IDX_DIM + IDX_DIM + N_IDX_HEADS
    assert s % TOKEN_TILE == 0 and (db * n) % TOKEN_TILE == 0 and s % BAND_Q_TILE == 0

    tabs_p = _all_rope_tables(jnp.arange(s))
    tabs_s = _all_rope_tables(jnp.tile(past + jnp.arange(n), db))
    row = lambda g: g.reshape(1, -1)

    yp = x_prompt.reshape(b * s, d)
    ys = x_sample.reshape(db * n, d)
    outs = [[] for _ in range(10)]
    for l in range(depth):
        last = l == depth - 1
        wg1, wu1, wd1 = (w[l].astype(bf16) for w in (ffn1_w_gate, ffn1_w_up, ffn1_w_down))
        wg2, wu2, wd2 = (w[l].astype(bf16) for w in (ffn2_w_gate, ffn2_w_up, ffn2_w_down))
        win = jnp.pad(w_in[l], ((0, 0), (0, 6 * D_HEADS + N_IDX_HEADS * IDX_DIM + LANES - d_in))).astype(bf16)
        wo = w_out[l].astype(bf16)
        g1, gm, g2 = row(ffn1_norm[l]), row(mix_norm[l]), row(ffn2_norm[l])
        gf = row(final_norm)

        (x1, qa, ka, va, qb, kb, vb, ka32, va32, kb32, vb32, qi, tail, kirep) = _ffn_inproj(
            yp, g1, wg1, wu1, wd1, gm, win, tabs_p, s // TOKEN_TILE)
        front = ((0, 0), (LEFT_CHUNKS * CHUNK, 0), (0, 0))
        oa = _band_prompt(qa, jnp.pad(ka.reshape(b, s, D_HEADS), front), jnp.pad(va.reshape(b, s, D_HEADS), front),
                          _band_prompt_bias(rel_bias[l]), b, s)
        ob = _dsa_prompt(qi, tail, qb, kirep, kb, vb, b, s, k_sel_p)
        yp = _out_ffn(x1, oa, ob, wo, g2, wg2, wu2, wd2, gf, last)
        heads = lambda a, rows: a.reshape(-1, rows, N_HEADS, HEAD_DIM)
        outs[0].append(heads(ka32, s)[:, s - a_keep:])
        outs[1].append(heads(va32, s)[:, s - a_keep:])
        outs[2].append(heads(kb32, s))
        outs[3].append(heads(vb32, s))
        outs[4].append(tail.reshape(b, s, LANES)[:, :, :IDX_DIM])

        (x1, qa, ka, va, qb, kb, vb, ka32, va32, kb32, vb32, qi, tail, kirep) = _ffn_inproj(
            ys, g1, wg1, wu1, wd1, gm, win, tabs_s, (db * n) // TOKEN_TILE)
        flat = lambda c: c.reshape(c.shape[0], c.shape[1], -1)
        oa, ob = _sample_mixers(qa, ka, va, qb, kb, vb, qi, tail,
                                flat(cache_a_k[l]), flat(cache_a_v[l]), flat(cache_b_k[l]), flat(cache_b_v[l]),
                                cache_idx_k[l], _band_sample_bias(rel_bias[l], n, a_len), db, n, k_sel_s)
        ys = _out_ffn(x1, oa, ob, wo, g2, wg2, wu2, wd2, gf, last)
        outs[5].append(heads(ka32, n))
        outs[6].append(heads(va32, n))
        outs[7].append(heads(kb32, n))
        outs[8].append(heads(vb32, n))
        outs[9].append(tail.reshape(db, n, LANES)[:, :, :IDX_DIM])

    st = [jnp.stack(o) for o in outs]
    return (yp.reshape(b, s, d), ys.reshape(db, n, d),
            st[0], st[1], st[2], st[3], st[4], st[5], st[6], st[7], st[8], st[9])
```
